```python
import jax, jax.numpy as jnp
from jax import lax
import numpy as np

D_MODEL = 1024
BATCH = 8
SEQ = 2048
DEPTH = 2
DEC_BATCH = 32
DEC_SEQ = 8
PAST_LEN = 16384
PAGE_SIZE = 128

N_A_LAYERS = DEPTH // 2
N_B_LAYERS = DEPTH - N_A_LAYERS
N_DENSE = (DEPTH + 1) // 2
N_MOE = DEPTH // 2
POOL_WINDOWS = (2, 4, 8, 16)
N_POOL_GROUPS = len(POOL_WINDOWS)
POOL_GROUP = D_MODEL // N_POOL_GROUPS
POOL_HIST = max(POOL_WINDOWS) - 1
N_HEADS = 8
HEAD_DIM = D_MODEL // N_HEADS
ATTN_SCALE = HEAD_DIM ** -0.5
MOBA_BLOCK = 256
MOBA_TOPK = 3
PROMPT_QBLOCK = 16
D_FF = 256 * ((8 * D_MODEL // 3 + 255) // 256)
N_EXPERTS = 8
TOP_K = 2
D_FF_EXPERT = 7 * D_MODEL // 2
EPS = 1e-6
NEG = -1e30

kernel_name = 'yoco_pool_moba_moe_step'


def rmsnorm(x, g):
    xf = x.astype(jnp.float32)
    y = xf * lax.rsqrt(jnp.mean(xf * xf, axis=-1, keepdims=True) + EPS)
    return (y * g.astype(jnp.float32)).astype(x.dtype)


def pool_mixer(xn, hist, pos, w_pool, pool_scale):
    B, T, _ = xn.shape
    seq = jnp.concatenate([hist.astype(xn.dtype), xn], axis=1).astype(jnp.float32)
    cs = jnp.cumsum(seq, axis=1)
    cs = jnp.concatenate([jnp.zeros_like(cs[:, :1]), cs], axis=1)
    hi = cs[:, POOL_HIST + 1:]
    groups = []
    for g, w in enumerate(POOL_WINDOWS):
        c0, c1 = g * POOL_GROUP, (g + 1) * POOL_GROUP
        lo = cs[:, POOL_HIST + 1 - w:POOL_HIST + 1 - w + T, c0:c1]
        cnt = jnp.minimum(pos + 1, w).astype(jnp.float32)[None, :, None]
        groups.append((hi[..., c0:c1] - lo) / cnt)
    diff = jnp.concatenate(groups, axis=-1) - seq[:, POOL_HIST:]
    y = jnp.einsum('btgc,gcd->btgd', diff.reshape(B, T, N_POOL_GROUPS, POOL_GROUP), w_pool.astype(jnp.float32))
    y = y.reshape(B, T, D_MODEL) * pool_scale.astype(jnp.float32)
    return y.astype(xn.dtype), seq[:, -POOL_HIST:].astype(xn.dtype)


def swiglu(x, wg, wu, wd):
    return (jax.nn.silu(x @ wg) * (x @ wu)) @ wd


def moe_swiglu(x, w_router, wg, wu, wd):
    logits = jnp.einsum('btd,de->bte', x, w_router).astype(jnp.float32)
    top_val, top_idx = lax.top_k(logits, TOP_K)
    gate = jnp.sum(jax.nn.one_hot(top_idx, N_EXPERTS, dtype=jnp.float32)
                   * jax.nn.softmax(top_val, axis=-1)[..., None], axis=-2)
    y = jnp.zeros(x.shape, jnp.float32)
    for e in range(N_EXPERTS):
        y = y + gate[..., e:e + 1] * swiglu(x, wg[e], wu[e], wd[e]).astype(jnp.float32)
    return y.astype(x.dtype)


def shared_kv(h, norm_kv, w_kv, k_norm):
    B, T, _ = h.shape
    kv = (rmsnorm(h, norm_kv) @ w_kv).reshape(B, T, 2, N_HEADS, HEAD_DIM)
    return rmsnorm(kv[:, :, 0], k_norm), kv[:, :, 1]


def queries(h, g_pre, w_q, q_norm):
    B, T, _ = h.shape
    return rmsnorm((rmsnorm(h, g_pre) @ w_q).reshape(B, T, N_HEADS, HEAD_DIM), q_norm)


def moba_select(q, qpos, means):
    own = qpos // MOBA_BLOCK
    s = jnp.einsum('bchd,bnhd->bchn', q.astype(jnp.float32), means)
    past = jnp.arange(means.shape[1])[None, :] < own[:, None]
    s = jnp.where(past[None, :, None, :], s, NEG)
    _, sel = lax.top_k(s, MOBA_TOPK)
    ok = sel < own[None, :, None, None]
    return sel, ok, own


def moba_positions(sel, ok, own, qpos):
    B, C, H, _ = sel.shape
    offs = jnp.arange(MOBA_BLOCK)
    sel_pos = (sel[..., None] * MOBA_BLOCK + offs).reshape(B, C, H, MOBA_TOPK * MOBA_BLOCK)
    sel_ok = jnp.repeat(ok, MOBA_BLOCK, axis=-1)
    own_pos = own[:, None] * MOBA_BLOCK + offs
    own_ok = own_pos <= qpos[:, None]
    shape = (B, C, H, MOBA_BLOCK)
    pos = jnp.concatenate([sel_pos, jnp.broadcast_to(own_pos[None, :, None], shape)], axis=-1)
    valid = jnp.concatenate([sel_ok, jnp.broadcast_to(own_ok[None, :, None], shape)], axis=-1)
    return pos, valid


def moba_prompt(q, k, v):
    B, S, H, D = q.shape
    nbp = max(-(-S // MOBA_BLOCK), MOBA_TOPK)
    kp = jnp.pad(k.astype(jnp.float32), ((0, 0), (0, nbp * MOBA_BLOCK - S), (0, 0), (0, 0)))
    means = jnp.sum(kp.reshape(B, nbp, MOBA_BLOCK, H, D), axis=2) / MOBA_BLOCK
    kt = k.transpose(0, 2, 1, 3)
    vt = v.transpose(0, 2, 1, 3)
    bi = jnp.arange(B)[:, None, None, None]
    hi = jnp.arange(H)[None, None, :, None]

    def block(args):
        qb, start = args
        qpos = start + jnp.arange(PROMPT_QBLOCK)
        sel, ok, own = moba_select(qb, qpos, means)
        pos, valid = moba_positions(sel, ok, own, qpos)
        idx = jnp.minimum(pos, S - 1)
        kr = kt[bi, hi, idx]
        vr = vt[bi, hi, idx]
        s = jnp.einsum('bchd,bchnd->bchn', qb, kr).astype(jnp.float32) * ATTN_SCALE
        p = jax.nn.softmax(jnp.where(valid, s, NEG), axis=-1)
        return jnp.einsum('bchn,bchnd->bchd', p, vr.astype(jnp.float32))

    nqb = S // PROMPT_QBLOCK
    qb = q.reshape(B, nqb, PROMPT_QBLOCK, H, D).transpose(1, 0, 2, 3, 4)
    out = lax.map(block, (qb, jnp.arange(nqb) * PROMPT_QBLOCK))
    return out.transpose(1, 0, 2, 3, 4).reshape(B, S, H * D)


def moba_sample(q, k_new, v_new, cache_k, cache_v, page_table):
    B, T, H, D = q.shape
    L = PAST_LEN + T
    nbp = max(-(-L // MOBA_BLOCK), MOBA_TOPK)
    n_pages = page_table.shape[1]
    page_sums = lax.map(lambda pt: jnp.sum(cache_k[pt].astype(jnp.float32), axis=1), page_table)
    page_blk = jax.nn.one_hot(jnp.arange(n_pages) * PAGE_SIZE // MOBA_BLOCK, nbp, dtype=jnp.float32)
    new_blk_id = (PAST_LEN + jnp.arange(T)) // MOBA_BLOCK
    new_blk = jax.nn.one_hot(new_blk_id, nbp, dtype=jnp.float32)
    means = (jnp.einsum('bphd,pn->bnhd', page_sums, page_blk)
             + jnp.einsum('bthd,tn->bnhd', k_new.astype(jnp.float32), new_blk)) / MOBA_BLOCK
    s_new_all = jnp.einsum('bthd,bshd->bths', q, k_new).astype(jnp.float32) * ATTN_SCALE
    bi = jnp.arange(B)[:, None, None, None]
    hi = jnp.arange(H)[None, None, :, None]
    v_new_f = v_new.astype(jnp.float32)

    def token(args):
        qt, t, s_new = args
        qpos = jnp.reshape(PAST_LEN + t, (1,))
        sel, ok, own = moba_select(qt, qpos, means)
        pos, valid = moba_positions(sel, ok, own, qpos)
        valid = valid & (pos < PAST_LEN)
        posc = jnp.minimum(pos, PAST_LEN - 1)
        phys = page_table[bi, posc // PAGE_SIZE]
        slot = posc % PAGE_SIZE
        kr = cache_k[phys, slot, hi]
        vr = cache_v[phys, slot, hi]
        s_past = jnp.einsum('bchd,bchnd->bchn', qt, kr).astype(jnp.float32) * ATTN_SCALE
        in_sel = jnp.any((sel[..., :, None] == new_blk_id) & ok[..., :, None], axis=-2)
        new_ok = (jnp.arange(T) <= t) & ((new_blk_id == own[0]) | in_sel)
        s = jnp.concatenate([jnp.where(valid, s_past, NEG), jnp.where(new_ok, s_new, NEG)], axis=-1)
        p = jax.nn.softmax(s, axis=-1)
        n = pos.shape[-1]
        return (jnp.einsum('bchn,bchnd->bchd', p[..., :n], vr.astype(jnp.float32))
                + jnp.einsum('bchs,bshd->bchd', p[..., n:], v_new_f))

    out = lax.map(token, (q.transpose(1, 0, 2, 3)[:, :, None], jnp.arange(T),
                          s_new_all.transpose(1, 0, 2, 3)[:, :, None]))
    return out[:, :, 0].transpose(1, 0, 2, 3).reshape(B, T, H * D)


def setup_inputs(seed: int = 0) -> dict:
    key = jax.random.key(seed)
    ks = jax.random.split(key, 24)
    f32 = jnp.float32
    n_pages = PAST_LEN // PAGE_SIZE
    n_pool = (DEC_BATCH * n_pages * 5) // 4
    hd = N_HEADS * HEAD_DIM

    def nrm(k, shape, scale):
        return jax.random.normal(k, shape, f32) * scale

    def gain(k, shape):
        return 1.0 + 0.1 * jax.random.normal(k, shape, f32)

    page_table = jax.random.permutation(ks[5], n_pool)[:DEC_BATCH * n_pages].reshape(DEC_BATCH, n_pages).astype(jnp.int32)
    return {
        'x_prompt': nrm(ks[0], (BATCH, SEQ, D_MODEL), 1.0),
        'x_sample': nrm(ks[1], (DEC_BATCH, DEC_SEQ, D_MODEL), 1.0),
        'cache_k': nrm(ks[2], (n_pool, PAGE_SIZE, N_HEADS, HEAD_DIM), 1.0),
        'cache_v': nrm(ks[3], (n_pool, PAGE_SIZE, N_HEADS, HEAD_DIM), 1.0),
        'state_pool': nrm(ks[4], (N_A_LAYERS, DEC_BATCH, POOL_HIST, D_MODEL), 1.0),
        'page_table': page_table,
        'norm_mix': gain(ks[6], (DEPTH, D_MODEL)),
        'w_pool': nrm(ks[7], (N_A_LAYERS, N_POOL_GROUPS, POOL_GROUP, POOL_GROUP), POOL_GROUP ** -0.5),
        'pool_scale': gain(ks[8], (N_A_LAYERS, D_MODEL)),
        'norm_kv': gain(ks[9], (D_MODEL,)),
        'w_kv': nrm(ks[10], (D_MODEL, 2 * hd), D_MODEL ** -0.5),
        'k_norm': gain(ks[11], (HEAD_DIM,)),
        'w_q': nrm(ks[12], (N_B_LAYERS, D_MODEL, hd), D_MODEL ** -0.5),
        'q_norm': gain(ks[13], (N_B_LAYERS, HEAD_DIM)),
        'w_o': nrm(ks[14], (N_B_LAYERS, hd, D_MODEL), hd ** -0.5),
        'norm_ffn': gain(ks[15], (DEPTH, D_MODEL)),
        'w_ff_gate': nrm(ks[16], (N_DENSE, D_MODEL, D_FF), D_MODEL ** -0.5),
        'w_ff_up': nrm(ks[17], (N_DENSE, D_MODEL, D_FF), D_MODEL ** -0.5),
        'w_ff_down': nrm(ks[18], (N_DENSE, D_FF, D_MODEL), D_FF ** -0.5),
        'w_router': nrm(ks[19], (N_MOE, D_MODEL, N_EXPERTS), D_MODEL ** -0.5),
        'w_exp_gate': nrm(ks[20], (N_MOE, N_EXPERTS, D_MODEL, D_FF_EXPERT), D_MODEL ** -0.5),
        'w_exp_up': nrm(ks[21], (N_MOE, N_EXPERTS, D_MODEL, D_FF_EXPERT), D_MODEL ** -0.5),
        'w_exp_down': nrm(ks[22], (N_MOE, N_EXPERTS, D_FF_EXPERT, D_MODEL), D_FF_EXPERT ** -0.5),
    }


def reference(x_prompt, x_sample, cache_k, cache_v, state_pool, page_table,
              norm_mix, w_pool, pool_scale, norm_kv, w_kv, k_norm, w_q, q_norm, w_o,
              norm_ffn, w_ff_gate, w_ff_up, w_ff_down, w_router, w_exp_gate, w_exp_up, w_exp_down):
    hp, hs = x_prompt, x_sample
    Bp, S, _ = x_prompt.shape
    T = x_sample.shape[1]
    pos_p = jnp.arange(S)
    pos_s = PAST_LEN + jnp.arange(T)
    pool_p, pool_s = [], []
    k_p = v_p = k_s = v_s = None
    for layer in range(DEPTH):
        if layer < N_A_LAYERS:
            a = layer
            zeros_hist = jnp.zeros((Bp, POOL_HIST, D_MODEL), hp.dtype)
            yp, hist_p = pool_mixer(rmsnorm(hp, norm_mix[layer]), zeros_hist, pos_p, w_pool[a], pool_scale[a])
            ys, hist_s = pool_mixer(rmsnorm(hs, norm_mix[layer]), state_pool[a], pos_s, w_pool[a], pool_scale[a])
            pool_p.append(hist_p)
            pool_s.append(hist_s)
        else:
            if layer == N_A_LAYERS:
                k_p, v_p = shared_kv(hp, norm_kv, w_kv, k_norm)
                k_s, v_s = shared_kv(hs, norm_kv, w_kv, k_norm)
            b = layer - N_A_LAYERS
            q_p = queries(hp, norm_mix[layer], w_q[b], q_norm[b])
            q_s = queries(hs, norm_mix[layer], w_q[b], q_norm[b])
            yp = moba_prompt(q_p, k_p, v_p).astype(hp.dtype) @ w_o[b]
            ys = moba_sample(q_s, k_s, v_s, cache_k, cache_v, page_table).astype(hs.dtype) @ w_o[b]
        hp = hp + yp
        hs = hs + ys
        i = layer // 2
        if layer % 2 == 0:
            hp = hp + swiglu(rmsnorm(hp, norm_ffn[layer]), w_ff_gate[i], w_ff_up[i], w_ff_down[i])
            hs = hs + swiglu(rmsnorm(hs, norm_ffn[layer]), w_ff_gate[i], w_ff_up[i], w_ff_down[i])
        else:
            hp = hp + moe_swiglu(rmsnorm(hp, norm_ffn[layer]), w_router[i], w_exp_gate[i], w_exp_up[i], w_exp_down[i])
            hs = hs + moe_swiglu(rmsnorm(hs, norm_ffn[layer]), w_router[i], w_exp_gate[i], w_exp_up[i], w_exp_down[i])
    pool_prompt = jnp.stack(pool_p)
    pool_sample = jnp.stack(pool_s)
    return (hp, hs, k_p, v_p, k_s, v_s, pool_prompt, pool_sample)
```

```python
import functools

import jax
import jax.numpy as jnp
from jax import lax
from jax.experimental import pallas as pl
from jax.experimental.pallas import tpu as pltpu

F32 = jnp.float32
BF16 = jnp.bfloat16

EPS = 1e-6
NEG = -1e30
POOL_WINDOWS = (2, 4, 8, 16)
HIST_ROWS = 16
MOBA_BLOCK = 256
MOBA_TOPK = 3
TOP_K = 2
VMEM_LIMIT = 56 * 1024 * 1024


def _cparams(sem):
    return pltpu.CompilerParams(dimension_semantics=sem, vmem_limit_bytes=VMEM_LIMIT)


def _rms(x, g):
    return x * lax.rsqrt(jnp.mean(x * x, axis=-1, keepdims=True) + EPS) * g


def _split_bf16(x):
    hi = x.astype(BF16)
    lo = (x - hi.astype(F32)).astype(BF16)
    return hi, lo


def _dot_nt(a, b):
    return lax.dot_general(a, b, (((1,), (1,)), ((), ())), preferred_element_type=F32)


def _dot_nt_split(a, b):
    a_hi, a_lo = _split_bf16(a)
    b_hi, b_lo = _split_bf16(b)
    return _dot_nt(a_hi, b_hi) + _dot_nt(a_hi, b_lo) + _dot_nt(a_lo, b_hi)


def _const_spec(shape):
    n = len(shape)
    return pl.BlockSpec(shape, lambda *_: (0,) * n, pipeline_mode=pl.Buffered(1))


def _pool_groups(ext_ref, xn, rows, cnt_of, wp_ref):
    d = xn.shape[-1]
    gw = d // len(POOL_WINDOWS)
    outs = []
    for g, w in enumerate(POOL_WINDOWS):
        c0 = g * gw
        acc = xn[:, c0:c0 + gw]
        for j in range(1, w):
            acc = acc + ext_ref[HIST_ROWS - j:HIST_ROWS - j + rows, c0:c0 + gw]
        diff = acc / cnt_of(w) - xn[:, c0:c0 + gw]
        outs.append(diff)
    return outs


def _pool_prompt_kernel(x_ref, g_ref, wp_ref, ps_ref, h_ref, st_ref, ext_ref, *, ts):
    s = pl.program_id(1)
    x = x_ref[0]
    xn = _rms(x, g_ref[...])

    @pl.when(s == 0)
    def _():
        ext_ref[0:HIST_ROWS, :] = jnp.zeros((HIST_ROWS, x.shape[-1]), F32)

    @pl.when(s > 0)
    def _():
        ext_ref[0:HIST_ROWS, :] = ext_ref[ts:ts + HIST_ROWS, :]

    ext_ref[HIST_ROWS:HIST_ROWS + ts, :] = xn
    pos = s * ts + lax.broadcasted_iota(jnp.int32, (ts, 1), 0)
    diffs = _pool_groups(ext_ref, xn, ts, lambda w: jnp.minimum(pos + 1, w).astype(F32), wp_ref)
    ys = [jnp.dot(df.astype(BF16), wp_ref[g], preferred_element_type=F32) for g, df in enumerate(diffs)]
    y = jnp.concatenate(ys, axis=-1) * ps_ref[...]
    h_ref[0] = x + y

    @pl.when(s == pl.num_programs(1) - 1)
    def _():
        st_ref[0] = ext_ref[ts:ts + HIST_ROWS, :]


def _pool_prompt(x, g, wp, ps, ts=512):
    b, s, d = x.shape
    gw = d // len(POOL_WINDOWS)
    return pl.pallas_call(
        functools.partial(_pool_prompt_kernel, ts=ts),
        grid=(b, s // ts),
        in_specs=[
            pl.BlockSpec((1, ts, d), lambda i, j: (i, j, 0)),
            _const_spec((1, d)),
            _const_spec((len(POOL_WINDOWS), gw, gw)),
            _const_spec((1, d)),
        ],
        out_specs=[
            pl.BlockSpec((1, ts, d), lambda i, j: (i, j, 0)),
            pl.BlockSpec((1, HIST_ROWS, d), lambda i, j: (i, 0, 0)),
        ],
        out_shape=[jax.ShapeDtypeStruct((b, s, d), F32), jax.ShapeDtypeStruct((b, HIST_ROWS, d), F32)],
        scratch_shapes=[pltpu.VMEM((HIST_ROWS + ts, d), F32)],
        compiler_params=_cparams(("arbitrary", "arbitrary")),
        name="pool_prompt",
    )(x, g, wp, ps)


def _pool_sample_kernel(ext_in_ref, x_ref, g_ref, wp_ref, ps_ref, h_ref, st_ref, ext_ref, diff_ref,
                        *, t, first_pos):
    nb = ext_in_ref.shape[0]
    gw = x_ref.shape[-1] // len(POOL_WINDOWS)

    def body(b, carry):
        e = ext_in_ref[b]
        xn = _rms(e[HIST_ROWS:HIST_ROWS + t], g_ref[...])
        ext_ref[0:HIST_ROWS, :] = e[0:HIST_ROWS]
        ext_ref[HIST_ROWS:HIST_ROWS + t, :] = xn
        pos = first_pos + lax.broadcasted_iota(jnp.int32, (t, 1), 0)
        diffs = _pool_groups(ext_ref, xn, t, lambda w: jnp.minimum(pos + 1, w).astype(F32), wp_ref)
        row = pl.multiple_of(b * t, t)
        for g, df in enumerate(diffs):
            diff_ref[pl.ds(row, t), g * gw:(g + 1) * gw] = df
        st_ref[b] = ext_ref[t:t + HIST_ROWS, :]
        return carry

    lax.fori_loop(0, nb, body, 0)
    ys = [jnp.dot(diff_ref[:, g * gw:(g + 1) * gw].astype(BF16), wp_ref[g], preferred_element_type=F32)
          for g in range(len(POOL_WINDOWS))]
    h_ref[...] = x_ref[...] + jnp.concatenate(ys, axis=-1) * ps_ref[...]


def _pool_sample(x, hist, g, wp, ps, first_pos):
    b, t, d = x.shape
    gw = d // len(POOL_WINDOWS)
    pad = jnp.zeros((b, HIST_ROWS - hist.shape[1], d), F32)
    ext_in = jnp.concatenate([pad, hist, x], axis=1)
    return pl.pallas_call(
        functools.partial(_pool_sample_kernel, t=t, first_pos=first_pos),
        grid=(1,),
        in_specs=[
            _const_spec((b, HIST_ROWS + t, d)),
            _const_spec((b * t, d)),
            _const_spec((1, d)),
            _const_spec((len(POOL_WINDOWS), gw, gw)),
            _const_spec((1, d)),
        ],
        out_specs=[
            pl.BlockSpec((b * t, d), lambda i: (0, 0)),
            pl.BlockSpec((b, HIST_ROWS, d), lambda i: (0, 0, 0)),
        ],
        out_shape=[jax.ShapeDtypeStruct((b * t, d), F32), jax.ShapeDtypeStruct((b, HIST_ROWS, d), F32)],
        scratch_shapes=[pltpu.VMEM((HIST_ROWS + t, d), F32), pltpu.VMEM((b * t, d), F32)],
        compiler_params=_cparams(("arbitrary",)),
        name="pool_sample",
    )(ext_in, x.reshape(b * t, d), g, wp, ps)


def _ffn_kernel(h_ref, g_ref, wg_ref, wu_ref, wd_ref, o_ref, *, n_chunks):
    h = h_ref[...]
    xn = _rms(h, g_ref[...]).astype(BF16)
    fc = wg_ref.shape[1] // n_chunks
    acc = h
    for c in range(n_chunks):
        gate = jnp.dot(xn, wg_ref[:, c * fc:(c + 1) * fc], preferred_element_type=F32)
        up = jnp.dot(xn, wu_ref[:, c * fc:(c + 1) * fc], preferred_element_type=F32)
        act = (gate * jax.nn.sigmoid(gate) * up).astype(BF16)
        acc = acc + jnp.dot(act, wd_ref[c * fc:(c + 1) * fc, :], preferred_element_type=F32)
    o_ref[...] = acc


def _ffn(h, g, wg, wu, wd, tm):
    n, d = h.shape
    f = wg.shape[1]
    return pl.pallas_call(
        functools.partial(_ffn_kernel, n_chunks=2),
        grid=(n // tm,),
        in_specs=[
            pl.BlockSpec((tm, d), lambda i: (i, 0)),
            _const_spec((1, d)),
            _const_spec((d, f)),
            _const_spec((d, f)),
            _const_spec((f, d)),
        ],
        out_specs=pl.BlockSpec((tm, d), lambda i: (i, 0)),
        out_shape=jax.ShapeDtypeStruct((n, d), F32),
        compiler_params=_cparams(("arbitrary",)),
        name="ffn_dense",
    )(h, g, wg, wu, wd)


def _kvq_kernel(h_ref, gkv_ref, gq_ref, kn_ref, qn_ref, wkv_ref, wq_ref, k_ref, v_ref, q_ref, *, hd):
    h = h_ref[...]
    d = h.shape[-1]
    r = h * lax.rsqrt(jnp.mean(h * h, axis=-1, keepdims=True) + EPS)
    kv = jnp.dot((r * gkv_ref[...]).astype(BF16), wkv_ref[...], preferred_element_type=F32)
    q = jnp.dot((r * gq_ref[...]).astype(BF16), wq_ref[...], preferred_element_type=F32)
    v_ref[...] = kv[:, d:]
    for hh in range(d // hd):
        sl = slice(hh * hd, (hh + 1) * hd)
        k_ref[:, sl] = _rms(kv[:, sl], kn_ref[...])
        q_ref[:, sl] = _rms(q[:, sl], qn_ref[...])


def _kvq(h, gkv, gq, kn, qn, wkv, wq, tm):
    n, d = h.shape
    hd = kn.shape[-1]
    out = jax.ShapeDtypeStruct((n, d), F32)
    tile = pl.BlockSpec((tm, d), lambda i: (i, 0))
    return pl.pallas_call(
        functools.partial(_kvq_kernel, hd=hd),
        grid=(n // tm,),
        in_specs=[tile, _const_spec((1, d)), _const_spec((1, d)), _const_spec((1, hd)), _const_spec((1, hd)),
                  _const_spec((d, 2 * d)), _const_spec((d, d))],
        out_specs=[tile, tile, tile],
        out_shape=[out, out, out],
        compiler_params=_cparams(("arbitrary",)),
        name="kvq_proj",
    )(h, gkv, gq, kn, qn, wkv, wq)


def _moba_prompt_kernel(q_ref, k_ref, v_ref, o_ref, *, scale):
    s_len, hd = k_ref.shape[1], k_ref.shape[2]
    nb = s_len // MOBA_BLOCK
    kf = k_ref[0]
    means = jnp.sum(kf.reshape(nb, MOBA_BLOCK, hd), axis=1) / MOBA_BLOCK
    kb = kf.astype(BF16)
    vb = v_ref[0].astype(BF16)
    row = lax.broadcasted_iota(jnp.int32, (MOBA_BLOCK, MOBA_BLOCK), 0)
    col = lax.broadcasted_iota(jnp.int32, (MOBA_BLOCK, MOBA_BLOCK), 1)
    causal = col <= row
    for qi in range(nb):
        qf = q_ref[0, qi * MOBA_BLOCK:(qi + 1) * MOBA_BLOCK, :]
        nk = (qi + 1) * MOBA_BLOCK
        s = _dot_nt(qf.astype(BF16), kb[:nk]) * scale
        pieces = []
        if qi > MOBA_TOPK:
            sc = _dot_nt_split(qf, means)
            cols = [sc[:, n:n + 1] for n in range(qi)]
            for n in range(qi):
                rank = jnp.zeros((MOBA_BLOCK, 1), jnp.int32)
                for m in range(qi):
                    if m == n:
                        continue
                    beats = (cols[m] >= cols[n]) if m < n else (cols[m] > cols[n])
                    rank = rank + beats.astype(jnp.int32)
                keep = jnp.broadcast_to(rank < MOBA_TOPK, (MOBA_BLOCK, MOBA_BLOCK))
                pieces.append(jnp.where(keep, s[:, n * MOBA_BLOCK:(n + 1) * MOBA_BLOCK], NEG))
        else:
            for n in range(qi):
                pieces.append(s[:, n * MOBA_BLOCK:(n + 1) * MOBA_BLOCK])
        pieces.append(jnp.where(causal, s[:, qi * MOBA_BLOCK:], NEG))
        sm = jnp.concatenate(pieces, axis=-1) if len(pieces) > 1 else pieces[0]
        m = jnp.max(sm, axis=-1, keepdims=True)
        p = jnp.exp(sm - m)
        l = jnp.sum(p, axis=-1, keepdims=True)
        o = jnp.dot(p.astype(BF16), vb[:nk], preferred_element_type=F32) / l
        o_ref[0, qi * MOBA_BLOCK:(qi + 1) * MOBA_BLOCK, :] = o.astype(o_ref.dtype)


def _moba_prompt(q, k, v, n_heads):
    b, s, d = q.shape
    hd = d // n_heads
    spec = pl.BlockSpec((1, s, hd), lambda i, j: (i, 0, j))
    return pl.pallas_call(
        functools.partial(_moba_prompt_kernel, scale=hd ** -0.5),
        grid=(b, n_heads),
        in_specs=[spec, spec, spec],
        out_specs=spec,
        out_shape=jax.ShapeDtypeStruct((b, s, d), BF16),
        compiler_params=_cparams(("arbitrary", "arbitrary")),
        name="moba_prompt",
    )(q, k, v)


def _sample_kpass_kernel(pt_ref, q_ref, *refs, pp):
    k_refs, (st_ref, ps_ref) = refs[:pp], refs[pp:]
    n_heads = q_ref.shape[1]
    for i in range(pp):
        kr = k_refs[i]
        page = kr.shape[1]
        ps_ref[0, i] = jnp.sum(kr[0], axis=0)
        for h in range(n_heads):
            kh = kr[0, :, h, :].astype(BF16)
            st_ref[0, h, :, i * page:(i + 1) * page] = _dot_nt(q_ref[0, h].astype(BF16), kh)


def _sample_kpass(q, cache_k, page_table, pp=8):
    b, n_heads, t, hd = q.shape
    _, page, _, _ = cache_k.shape
    n_pages = page_table.shape[1]

    def k_map(i):
        return lambda bi, j, pt: (pt[bi, j * pp + i], 0, 0, 0)

    grid_spec = pltpu.PrefetchScalarGridSpec(
        num_scalar_prefetch=1,
        grid=(b, n_pages // pp),
        in_specs=[pl.BlockSpec((1, n_heads, t, hd), lambda bi, j, pt: (bi, 0, 0, 0))]
        + [pl.BlockSpec((1, page, n_heads, hd), k_map(i)) for i in range(pp)],
        out_specs=[
            pl.BlockSpec((1, n_heads, t, pp * page), lambda bi, j, pt: (bi, 0, 0, j)),
            pl.BlockSpec((1, pp, n_heads, hd), lambda bi, j, pt: (bi, j, 0, 0)),
        ],
    )
    return pl.pallas_call(
        functools.partial(_sample_kpass_kernel, pp=pp),
        grid_spec=grid_spec,
        out_shape=[jax.ShapeDtypeStruct((b, n_heads, t, n_pages * page), F32),
                   jax.ShapeDtypeStruct((b, n_pages, n_heads, hd), F32)],
        compiler_params=_cparams(("arbitrary", "arbitrary")),
        name="sample_kpass",
    )(page_table, q, *([cache_k] * pp))


def _sample_select_kernel(st_ref, ps_ref, q_ref, kn_ref, vn_ref, p_ref, on_ref, *, scale, past_len, page):
    n_heads, t, n_keys = st_ref.shape[1], st_ref.shape[2], st_ref.shape[3]
    hd = q_ref.shape[-1]
    ppb = MOBA_BLOCK // page
    nbc = n_keys // MOBA_BLOCK
    nbp = max(-(-(past_len + t) // MOBA_BLOCK), MOBA_TOPK)
    nbl = -(-nbp // 128) * 128
    new_blk = [(past_len + i) // MOBA_BLOCK for i in range(t)]
    key_blk = lax.broadcasted_iota(jnp.int32, (nbl, n_keys), 1) // MOBA_BLOCK
    expand = (key_blk == lax.broadcasted_iota(jnp.int32, (nbl, n_keys), 0)).astype(BF16)
    blk_id = lax.broadcasted_iota(jnp.int32, (t, nbl), 1)
    blk_row = lax.broadcasted_iota(jnp.int32, (nbl, 1), 0)
    own = jnp.concatenate([jnp.full((1, 1), nb, jnp.int32) for nb in new_blk], axis=0)
    tok_r = lax.broadcasted_iota(jnp.int32, (t, t), 0)
    tok_c = lax.broadcasted_iota(jnp.int32, (t, t), 1)
    for h in range(n_heads):
        qh = q_ref[0, h]
        knh = kn_ref[0, h]
        vnh = vn_ref[0, h]
        psum = ps_ref[0, :, h, :]
        bsum = jnp.sum(psum.reshape(nbc, ppb, hd), axis=1)
        bsum = jnp.concatenate([bsum, jnp.zeros((nbl - nbc, hd), F32)], axis=0)
        for i, nb in enumerate(new_blk):
            bsum = bsum + jnp.where(blk_row == nb, knh[i:i + 1], 0.0)
        means = bsum / MOBA_BLOCK
        sc = _dot_nt_split(qh, means)
        past = blk_id < own
        sc = jnp.where(past, sc, NEG)
        sel = jnp.zeros((t, nbl), F32)
        for _ in range(MOBA_TOPK):
            mx = jnp.max(sc, axis=-1, keepdims=True)
            first = jnp.min(jnp.where(sc == mx, blk_id, nbl), axis=-1, keepdims=True)
            sel = jnp.where((blk_id == first) & past, 1.0, sel)
            sc = jnp.where(blk_id == first, -jnp.inf, sc)
        keep = jnp.dot(sel.astype(BF16), expand, preferred_element_type=F32) > 0.5
        s_past = jnp.where(keep, st_ref[0, h] * scale, NEG)
        new_ok = jnp.zeros((t, t), jnp.bool_)
        for si, nb in enumerate(new_blk):
            vis = (own == nb) | (jnp.sum(jnp.where(blk_id == nb, sel, 0.0), axis=-1, keepdims=True) > 0.5)
            new_ok = new_ok | ((tok_c == si) & vis)
        new_ok = new_ok & (tok_c <= tok_r)
        s_new = jnp.where(new_ok, _dot_nt(qh.astype(BF16), knh.astype(BF16)) * scale, NEG)
        m = jnp.maximum(jnp.max(s_past, axis=-1, keepdims=True), jnp.max(s_new, axis=-1, keepdims=True))
        p_past = jnp.exp(s_past - m)
        p_new = jnp.exp(s_new - m)
        l = jnp.sum(p_past, axis=-1, keepdims=True) + jnp.sum(p_new, axis=-1, keepdims=True)
        p_ref[0, h] = p_past / l
        on_ref[0, :, h * hd:(h + 1) * hd] = jnp.dot((p_new / l).astype(BF16), vnh.astype(BF16),
                                                    preferred_element_type=F32)


def _sample_select(st, psums, q, k_new, v_new, past_len, page):
    b, n_heads, t, n_keys = st.shape
    hd = q.shape[-1]
    n_pages = psums.shape[1]
    tok = pl.BlockSpec((1, n_heads, t, hd), lambda i: (i, 0, 0, 0))
    return pl.pallas_call(
        functools.partial(_sample_select_kernel, scale=hd ** -0.5, past_len=past_len, page=page),
        grid=(b,),
        in_specs=[
            pl.BlockSpec((1, n_heads, t, n_keys), lambda i: (i, 0, 0, 0)),
            pl.BlockSpec((1, n_pages, n_heads, hd), lambda i: (i, 0, 0, 0)),
            tok, tok, tok,
        ],
        out_specs=[
            pl.BlockSpec((1, n_heads, t, n_keys), lambda i: (i, 0, 0, 0)),
            pl.BlockSpec((1, t, n_heads * hd), lambda i: (i, 0, 0)),
        ],
        out_shape=[jax.ShapeDtypeStruct((b, n_heads, t, n_keys), F32),
                   jax.ShapeDtypeStruct((b, t, n_heads * hd), F32)],
        compiler_params=_cparams(("arbitrary",)),
        name="sample_select",
    )(st, psums, q, k_new, v_new)


def _sample_vpass_kernel(pt_ref, p_ref, on_ref, *refs, pp):
    v_refs, o_ref = refs[:pp], refs[pp]
    n_heads = p_ref.shape[1]
    j = pl.program_id(1)

    @pl.when(j == 0)
    def _():
        o_ref[...] = on_ref[...]

    page = v_refs[0].shape[1]
    hd = v_refs[0].shape[3]
    for h in range(n_heads):
        acc = jnp.zeros((p_ref.shape[2], hd), F32)
        for i in range(pp):
            ph = p_ref[0, h, :, i * page:(i + 1) * page].astype(BF16)
            vh = v_refs[i][0, :, h, :].astype(BF16)
            acc = acc + jnp.dot(ph, vh, preferred_element_type=F32)
        o_ref[0, :, h * hd:(h + 1) * hd] += acc


def _sample_vpass(p, o_new, cache_v, page_table, pp=8):
    b, n_heads, t, n_keys = p.shape
    _, page, _, hd = cache_v.shape
    n_pages = page_table.shape[1]

    def v_map(i):
        return lambda bi, j, pt: (pt[bi, j * pp + i], 0, 0, 0)

    grid_spec = pltpu.PrefetchScalarGridSpec(
        num_scalar_prefetch=1,
        grid=(b, n_pages // pp),
        in_specs=[pl.BlockSpec((1, n_heads, t, pp * page), lambda bi, j, pt: (bi, 0, 0, j)),
                  pl.BlockSpec((1, t, n_heads * hd), lambda bi, j, pt: (bi, 0, 0))]
        + [pl.BlockSpec((1, page, n_heads, hd), v_map(i)) for i in range(pp)],
        out_specs=pl.BlockSpec((1, t, n_heads * hd), lambda bi, j, pt: (bi, 0, 0)),
    )
    return pl.pallas_call(
        functools.partial(_sample_vpass_kernel, pp=pp),
        grid_spec=grid_spec,
        out_shape=jax.ShapeDtypeStruct((b, t, n_heads * hd), F32),
        compiler_params=_cparams(("arbitrary", "arbitrary")),
        name="sample_vpass",
    )(page_table, p, o_new, *([cache_v] * pp))


def _oproj_router_kernel(a_ref, h_ref, wo_ref, g_ref, wr_ref, h3_ref, xn_ref, gate_ref):
    h3 = h_ref[...] + jnp.dot(a_ref[...].astype(BF16), wo_ref[...], preferred_element_type=F32)
    h3_ref[...] = h3
    xn = _rms(h3, g_ref[...])
    xn_ref[...] = xn.astype(BF16)
    logits = _dot_nt_split(xn, wr_ref[...])
    ne = logits.shape[-1]
    eid = lax.broadcasted_iota(jnp.int32, logits.shape, 1)
    m1 = jnp.max(logits, axis=-1, keepdims=True)
    i1 = jnp.min(jnp.where(logits == m1, eid, ne), axis=-1, keepdims=True)
    rest = jnp.where(eid == i1, -jnp.inf, logits)
    m2 = jnp.max(rest, axis=-1, keepdims=True)
    i2 = jnp.min(jnp.where(rest == m2, eid, ne), axis=-1, keepdims=True)
    e2 = jnp.exp(m2 - m1)
    den = 1.0 + e2
    gate_ref[...] = jnp.where(eid == i1, 1.0 / den, 0.0) + jnp.where(eid == i2, e2 / den, 0.0)


def _oproj_router(a, h, wo, g, wr_t, tm):
    n, d = h.shape
    ne = wr_t.shape[0]
    tile = pl.BlockSpec((tm, d), lambda i: (i, 0))
    return pl.pallas_call(
        _oproj_router_kernel,
        grid=(n // tm,),
        in_specs=[tile, tile, _const_spec((d, d)), _const_spec((1, d)), _const_spec((ne, d))],
        out_specs=[tile, tile, pl.BlockSpec((tm, ne), lambda i: (i, 0))],
        out_shape=[jax.ShapeDtypeStruct((n, d), F32), jax.ShapeDtypeStruct((n, d), BF16),
                   jax.ShapeDtypeStruct((n, ne), F32)],
        compiler_params=_cparams(("arbitrary",)),
        name="oproj_router",
    )(a, h, wo, g, wr_t)


def _moe_kernel(x_ref, gate_ref, h_ref, wg_ref, wu_ref, wd_ref, o_ref, acc_ref):
    e = pl.program_id(1)
    f = pl.program_id(2)

    @pl.when((e == 0) & (f == 0))
    def _():
        acc_ref[...] = h_ref[...]

    x = x_ref[...]
    gate = jnp.dot(x, wg_ref[0], preferred_element_type=F32)
    up = jnp.dot(x, wu_ref[0], preferred_element_type=F32)
    act = (gate * jax.nn.sigmoid(gate) * up).astype(BF16)
    y = jnp.dot(act, wd_ref[0], preferred_element_type=F32)
    gates = gate_ref[...]
    eid = lax.broadcasted_iota(jnp.int32, gates.shape, 1)
    ge = jnp.sum(jnp.where(eid == e, gates, 0.0), axis=-1, keepdims=True)
    acc_ref[...] += ge * y

    @pl.when((e == pl.num_programs(1) - 1) & (f == pl.num_programs(2) - 1))
    def _():
        o_ref[...] = acc_ref[...]


def _moe(xn, gates, h, wg, wu, wd, tm, nf=4):
    n, d = h.shape
    ne, _, ff = wg.shape
    tf = ff // nf
    return pl.pallas_call(
        _moe_kernel,
        grid=(n // tm, ne, nf),
        in_specs=[
            pl.BlockSpec((tm, d), lambda i, e, f: (i, 0)),
            pl.BlockSpec((tm, ne), lambda i, e, f: (i, 0)),
            pl.BlockSpec((tm, d), lambda i, e, f: (i, 0)),
            pl.BlockSpec((1, d, tf), lambda i, e, f: (e, 0, f)),
            pl.BlockSpec((1, d, tf), lambda i, e, f: (e, 0, f)),
            pl.BlockSpec((1, tf, d), lambda i, e, f: (e, f, 0)),
        ],
        out_specs=pl.BlockSpec((tm, d), lambda i, e, f: (i, 0)),
        out_shape=jax.ShapeDtypeStruct((n, d), F32),
        scratch_shapes=[pltpu.VMEM((tm, d), F32)],
        compiler_params=_cparams(("arbitrary", "arbitrary", "arbitrary")),
        name="moe_dense",
    )(xn, gates, h, wg, wu, wd)


def kernel(x_prompt, x_sample, cache_k, cache_v, state_pool, page_table, norm_mix, w_pool, pool_scale, norm_kv, w_kv, k_norm, w_q, q_norm, w_o, norm_ffn, w_ff_gate, w_ff_up, w_ff_down, w_router, w_exp_gate, w_exp_up, w_exp_down):
    bp, s, d = x_prompt.shape
    bs, t, _ = x_sample.shape
    n_heads = cache_k.shape[2]
    hd = cache_k.shape[3]
    page = cache_k.shape[1]
    past_len = page_table.shape[1] * page
    n_hist = state_pool.shape[2]
    row = lambda v: v.reshape(1, -1)

    wp = w_pool[0].astype(BF16)
    hp, st_p = _pool_prompt(x_prompt, row(norm_mix[0]), wp, row(pool_scale[0]))
    hs, st_s = _pool_sample(x_sample, state_pool[0], row(norm_mix[0]), wp, row(pool_scale[0]), past_len)
    pool_prompt = st_p[None, :, HIST_ROWS - n_hist:]
    pool_sample = st_s[None, :, HIST_ROWS - n_hist:]
    wg, wu, wd = w_ff_gate[0].astype(BF16), w_ff_up[0].astype(BF16), w_ff_down[0].astype(BF16)
    hp = _ffn(hp.reshape(bp * s, d), row(norm_ffn[0]), wg, wu, wd, tm=512)
    hs = _ffn(hs, row(norm_ffn[0]), wg, wu, wd, tm=bs * t)

    wkv, wq = w_kv.astype(BF16), w_q[0].astype(BF16)
    proj = functools.partial(_kvq, gkv=row(norm_kv), gq=row(norm_mix[1]), kn=row(k_norm), qn=row(q_norm[0]),
                             wkv=wkv, wq=wq)
    k_p, v_p, q_p = proj(hp, tm=1024)
    k_s, v_s, q_s = proj(hs, tm=bs * t)

    a_p = _moba_prompt(q_p.reshape(bp, s, d), k_p.reshape(bp, s, d), v_p.reshape(bp, s, d), n_heads)
    q_s4 = q_s.reshape(bs, t, n_heads, hd)
    k_s4 = k_s.reshape(bs, t, n_heads, hd)
    v_s4 = v_s.reshape(bs, t, n_heads, hd)
    hm = lambda a: a.transpose(0, 2, 1, 3)
    st, psums = _sample_kpass(hm(q_s4), cache_k, page_table)
    probs, o_new = _sample_select(st, psums, hm(q_s4), hm(k_s4), hm(v_s4), past_len, page)
    a_s = _sample_vpass(probs, o_new, cache_v, page_table)

    wo = w_o[0].astype(BF16)
    wr_t = w_router[0].T
    weg, weu, wed = w_exp_gate[0].astype(BF16), w_exp_up[0].astype(BF16), w_exp_down[0].astype(BF16)
    h3p, xnp, gp = _oproj_router(a_p.reshape(bp * s, d), hp, wo, row(norm_ffn[1]), wr_t, tm=1024)
    h3s, xns, gs = _oproj_router(a_s.reshape(bs * t, d), hs, wo, row(norm_ffn[1]), wr_t, tm=bs * t)
    y_p = _moe(xnp, gp, h3p, weg, weu, wed, tm=1024)
    y_s = _moe(xns, gs, h3s, weg, weu, wed, tm=bs * t)

    return (y_p.reshape(bp, s, d), y_s.reshape(bs, t, d),
            k_p.reshape(bp, s, n_heads, hd), v_p.reshape(bp, s, n_heads, hd), k_s4, v_s4,
            pool_prompt, pool_sample)
```
